```python
import jax, jax.numpy as jnp
from jax import lax
import numpy as np

D_MODEL = 1024
BATCH = 2
SEQ = 8192
DEPTH = 1
DEC_BATCH = 8
DEC_SEQ = 16
PAST_LEN = 4096

CHUNK = 64
HEAD_DIM = 64
A_HEADS = 6
A_LEFT_CHUNKS = 8
A_REACH = A_LEFT_CHUNKS * CHUNK
REL_CLIP = 128
B_HEADS = 6
B_KV_HEADS = 2
B_GROUP = B_HEADS // B_KV_HEADS
B_WINDOW = 128
B_LEFT_CHUNKS = B_WINDOW // CHUNK
B_REACH = B_LEFT_CHUNKS * CHUNK
M_HEADS = 4
N_MEM = 256
A_W = A_HEADS * HEAD_DIM
B_W = B_HEADS * HEAD_DIM
B_KVW = B_KV_HEADS * HEAD_DIM
M_W = M_HEADS * HEAD_DIM
D_MIX = A_W + B_W + M_W
SPLIT_SIZES = (A_W, A_W, A_W, A_W, B_W, B_KVW, B_KVW, B_W, M_W, M_W)
D_IN = sum(SPLIT_SIZES)
EPS = 1e-6
NEG = -1e30
SCALE = HEAD_DIM ** -0.5

kernel_name = "hybrid_chunk_streaming_encoder_step"


def _rmsnorm(x, g):
    xf = x.astype(jnp.float32)
    xf = xf * lax.rsqrt(jnp.mean(xf * xf, axis=-1, keepdims=True) + EPS)
    return (xf * g.astype(jnp.float32)).astype(x.dtype)


def _heads(t, h):
    return t.reshape(*t.shape[:-1], h, HEAD_DIM)


def _alibi_slopes():
    return 2.0 ** (-8.0 * jnp.arange(1, B_HEADS + 1, dtype=jnp.float32) / B_HEADS)


def _relpos_bias(table, d):
    return table[:, jnp.clip(d, -REL_CLIP, REL_CLIP) + REL_CLIP].astype(jnp.float32)[:, None]


def _alibi_bias(d):
    b = -_alibi_slopes()[:, None, None] * jnp.abs(d).astype(jnp.float32)[None]
    return b.reshape(B_KV_HEADS, B_GROUP, *d.shape)


def _attn_core(q, k, v, bias, valid, sink):
    s = jnp.einsum('...qhgd,...khd->...hgqk', q, k).astype(jnp.float32) * SCALE
    if bias is not None:
        s = s + bias
    if valid is not None:
        s = jnp.where(valid[..., None, None, None, :], s, NEG)
    if sink is None:
        p = jax.nn.softmax(s, axis=-1)
    else:
        sk = sink.astype(jnp.float32).reshape(q.shape[-3], q.shape[-2], 1, 1)
        m = jnp.maximum(jnp.max(s, axis=-1, keepdims=True), sk)
        e = jnp.exp(s - m)
        p = e / (jnp.sum(e, axis=-1, keepdims=True) + jnp.exp(sk - m))
    return jnp.einsum('...hgqk,...khd->...qhgd', p.astype(v.dtype), v)


def _band_attention(q, k, v, left_chunks, bias, sink):
    b, s = q.shape[:2]
    nc = s // CHUNK
    qc = q.reshape(b, nc, CHUNK, *q.shape[2:])
    kc = k.reshape(b, nc, CHUNK, *k.shape[2:])
    vc = v.reshape(b, nc, CHUNK, *v.shape[2:])
    band = jnp.arange(nc)[:, None] + jnp.arange(-left_chunks, 1)[None, :]
    valid = jnp.repeat(band >= 0, CHUNK, axis=1)
    band = jnp.maximum(band, 0)
    n_keys = (left_chunks + 1) * CHUNK
    kb = kc[:, band].reshape(b, nc, n_keys, *k.shape[2:])
    vb = vc[:, band].reshape(b, nc, n_keys, *v.shape[2:])
    o = _attn_core(qc, kb, vb, bias, valid, sink)
    return o.reshape(b, s, -1)


def _band_distance(left_chunks):
    n_keys = (left_chunks + 1) * CHUNK
    return left_chunks * CHUNK + jnp.arange(CHUNK)[:, None] - jnp.arange(n_keys)[None, :]


def _project(x, g_pre, w_in):
    z = _rmsnorm(x, g_pre) @ w_in
    return jnp.split(z, np.cumsum(SPLIT_SIZES)[:-1].tolist(), axis=-1)


def _mem_kv(mem, g_mem, w_mem_kv):
    mk, mv = jnp.split(_rmsnorm(mem, g_mem) @ w_mem_kv, 2, axis=-1)
    return _heads(mk, M_HEADS), _heads(mv, M_HEADS)


def _merge(x, oa, ga, ob, gb, om, gm, w_out, g_post):
    o = jnp.concatenate([oa * jax.nn.silu(ga), ob * jax.nn.silu(gb), om * jax.nn.silu(gm)], axis=-1) @ w_out
    return x + _rmsnorm(o, g_post)


def _gqa_q(t):
    return _heads(t, B_HEADS).reshape(*t.shape[:-1], B_KV_HEADS, B_GROUP, HEAD_DIM)


def setup_inputs(seed: int = 0) -> dict:
    key = jax.random.key(seed)
    ks = jax.random.split(key, 17)
    f32 = jnp.float32
    la = min(A_REACH, PAST_LEN)
    lb = min(B_REACH, PAST_LEN)
    return {
        "x_prompt": jax.random.normal(ks[0], (BATCH, SEQ, D_MODEL), f32),
        "x_sample": jax.random.normal(ks[1], (DEC_BATCH, DEC_SEQ, D_MODEL), f32),
        "cache_a_k": jax.random.normal(ks[2], (DEPTH, DEC_BATCH, la, A_HEADS, HEAD_DIM), f32),
        "cache_a_v": jax.random.normal(ks[3], (DEPTH, DEC_BATCH, la, A_HEADS, HEAD_DIM), f32),
        "cache_b_k": jax.random.normal(ks[4], (DEPTH, DEC_BATCH, lb, B_KV_HEADS, HEAD_DIM), f32),
        "cache_b_v": jax.random.normal(ks[5], (DEPTH, DEC_BATCH, lb, B_KV_HEADS, HEAD_DIM), f32),
        "cache_mem_k": jax.random.normal(ks[6], (DEPTH, DEC_BATCH, N_MEM, M_HEADS, HEAD_DIM), f32),
        "cache_mem_v": jax.random.normal(ks[7], (DEPTH, DEC_BATCH, N_MEM, M_HEADS, HEAD_DIM), f32),
        "mem_prompt": jax.random.normal(ks[8], (BATCH, N_MEM, D_MODEL), f32),
        "g_pre": 1.0 + 0.02 * jax.random.normal(ks[9], (DEPTH, D_MODEL), f32),
        "w_in": jax.random.normal(ks[10], (DEPTH, D_MODEL, D_IN), f32) * D_MODEL ** -0.5,
        "rel_bias_a": 0.1 * jax.random.normal(ks[11], (DEPTH, A_HEADS, 2 * REL_CLIP + 1), f32),
        "sink_b": 0.5 * jax.random.normal(ks[12], (DEPTH, B_HEADS), f32),
        "g_mem": 1.0 + 0.02 * jax.random.normal(ks[13], (DEPTH, D_MODEL), f32),
        "w_mem_kv": jax.random.normal(ks[14], (DEPTH, D_MODEL, 2 * M_W), f32) * D_MODEL ** -0.5,
        "w_out": jax.random.normal(ks[15], (DEPTH, D_MIX, D_MODEL), f32) * D_MIX ** -0.5,
        "g_post": 1.0 + 0.02 * jax.random.normal(ks[16], (DEPTH, D_MODEL), f32),
    }


def reference(x_prompt, x_sample, cache_a_k, cache_a_v, cache_b_k, cache_b_v, cache_mem_k, cache_mem_v,
              mem_prompt, g_pre, w_in, rel_bias_a, sink_b, g_mem, w_mem_kv, w_out, g_post):
    seq = x_prompt.shape[1]
    n_new = x_sample.shape[1]
    la_p = min(A_REACH, seq)
    lb_p = min(B_REACH, seq)
    la_s = cache_a_k.shape[2]
    lb_s = cache_b_k.shape[2]

    d_a_prompt = _band_distance(A_LEFT_CHUNKS)
    d_b_prompt = _band_distance(B_LEFT_CHUNKS)
    q_pos = PAST_LEN + jnp.arange(n_new)
    k_pos_a = jnp.concatenate([PAST_LEN - la_s + jnp.arange(la_s), q_pos])
    k_pos_b = jnp.concatenate([PAST_LEN - lb_s + jnp.arange(lb_s), q_pos])
    d_a_sample = q_pos[:, None] - k_pos_a[None, :]
    d_b_sample = q_pos[:, None] - k_pos_b[None, :]
    bias_b_prompt = _alibi_bias(d_b_prompt)
    bias_b_sample = _alibi_bias(d_b_sample)

    yp, ys = x_prompt, x_sample
    akp, avp, bkp, bvp, mkp, mvp, aks, avs, bks, bvs = ([] for _ in range(10))
    for l in range(DEPTH):
        qa, ka, va, ga, qb, kb, vb, gb, qm, gm = _project(yp, g_pre[l], w_in[l])
        ka, va = _heads(ka, A_HEADS), _heads(va, A_HEADS)
        kb, vb = _heads(kb, B_KV_HEADS), _heads(vb, B_KV_HEADS)
        oa = _band_attention(_heads(qa, A_HEADS)[..., None, :], ka, va, A_LEFT_CHUNKS,
                             _relpos_bias(rel_bias_a[l], d_a_prompt), None)
        ob = _band_attention(_gqa_q(qb), kb, vb, B_LEFT_CHUNKS, bias_b_prompt, sink_b[l])
        mk, mv = _mem_kv(mem_prompt, g_mem[l], w_mem_kv[l])
        om = _attn_core(_heads(qm, M_HEADS)[..., None, :], mk, mv, None, None, None)
        om = om.reshape(*om.shape[:2], M_W)
        yp = _merge(yp, oa, ga, ob, gb, om, gm, w_out[l], g_post[l])
        akp.append(ka[:, seq - la_p:]); avp.append(va[:, seq - la_p:])
        bkp.append(kb[:, seq - lb_p:]); bvp.append(vb[:, seq - lb_p:])
        mkp.append(mk); mvp.append(mv)

        qa, ka, va, ga, qb, kb, vb, gb, qm, gm = _project(ys, g_pre[l], w_in[l])
        ka, va = _heads(ka, A_HEADS), _heads(va, A_HEADS)
        kb, vb = _heads(kb, B_KV_HEADS), _heads(vb, B_KV_HEADS)
        ka_all = jnp.concatenate([cache_a_k[l].astype(ka.dtype), ka], axis=1)
        va_all = jnp.concatenate([cache_a_v[l].astype(va.dtype), va], axis=1)
        kb_all = jnp.concatenate([cache_b_k[l].astype(kb.dtype), kb], axis=1)
        vb_all = jnp.concatenate([cache_b_v[l].astype(vb.dtype), vb], axis=1)
        oa = _attn_core(_heads(qa, A_HEADS)[..., None, :], ka_all, va_all,
                        _relpos_bias(rel_bias_a[l], d_a_sample), None, None)
        oa = oa.reshape(*oa.shape[:2], A_W)
        ob = _attn_core(_gqa_q(qb), kb_all, vb_all, bias_b_sample, None, sink_b[l])
        ob = ob.reshape(*ob.shape[:2], B_W)
        om = _attn_core(_heads(qm, M_HEADS)[..., None, :], cache_mem_k[l], cache_mem_v[l], None, None, None)
        om = om.reshape(*om.shape[:2], M_W)
        ys = _merge(ys, oa, ga, ob, gb, om, gm, w_out[l], g_post[l])
        aks.append(ka); avs.append(va); bks.append(kb); bvs.append(vb)

    state_a_k_prompt = jnp.stack(akp)
    state_a_v_prompt = jnp.stack(avp)
    state_b_k_prompt = jnp.stack(bkp)
    state_b_v_prompt = jnp.stack(bvp)
    state_mem_k_prompt = jnp.stack(mkp)
    state_mem_v_prompt = jnp.stack(mvp)
    state_a_k_sample = jnp.stack(aks)
    state_a_v_sample = jnp.stack(avs)
    state_b_k_sample = jnp.stack(bks)
    state_b_v_sample = jnp.stack(bvs)
    return (yp, ys, state_a_k_prompt, state_a_v_prompt, state_b_k_prompt, state_b_v_prompt,
            state_mem_k_prompt, state_mem_v_prompt, state_a_k_sample, state_a_v_sample,
            state_b_k_sample, state_b_v_sample)
```

```python
import functools

import jax
import jax.numpy as jnp
from jax import lax
from jax.experimental import pallas as pl
from jax.experimental.pallas import tpu as pltpu

F32 = jnp.float32
BF16 = jnp.bfloat16

D_MODEL = 1024
CHUNK = 64
HEAD_DIM = 64
LANES = 128
A_HEADS = 6
A_LEFT = 8
A_REACH = A_LEFT * CHUNK
REL_CLIP = 128
B_HEADS = 6
B_KV_HEADS = 2
B_GROUP = B_HEADS // B_KV_HEADS
B_LEFT = 2
B_REACH = B_LEFT * CHUNK
M_HEADS = 4
N_MEM = 256
A_W = A_HEADS * HEAD_DIM
B_W = B_HEADS * HEAD_DIM
B_KVW = B_KV_HEADS * HEAD_DIM
M_W = M_HEADS * HEAD_DIM
D_MIX = A_W + B_W + M_W
EPS = 1e-6
NEG = -1e30
SCALE = HEAD_DIM ** -0.5

C_QA = 0
C_KA = C_QA + A_W
C_VA = C_KA + A_W
C_GA = C_VA + A_W
C_QB = C_GA + A_W
C_KB = C_QB + B_W
C_VB = C_KB + B_KVW
C_GB = C_VB + B_KVW
C_QM = C_GB + B_W
C_GM = C_QM + M_W
D_IN = C_GM + M_W

TILE = 512
HIST = A_REACH
GA_ROWS = 256
GB_ROWS = 128
GM_ROWS = 256
WIN_A = GA_ROWS + A_REACH
WIN_B = GB_ROWS + B_REACH

S_KEYS_A = 640
S_KEYS_B = 256

VMEM_LIMIT_BYTES = 56 * 1024 * 1024


def _rms(x, g):
    ms = jnp.mean(x * x, axis=-1, keepdims=True)
    return x * lax.rsqrt(ms + EPS) * g


def _silu(g):
    return g / (1.0 + jnp.exp(-g))


def _attend(qz, k, v, bias, sink=None):
    s = lax.dot_general(qz, k, (((1,), (1,)), ((), ())), preferred_element_type=F32)
    if bias is not None:
        s = s + bias
    m = jnp.max(s, axis=-1, keepdims=True)
    if sink is not None:
        m = jnp.maximum(m, sink)
    e = jnp.exp(s - m)
    l = jnp.sum(e, axis=-1, keepdims=True)
    if sink is not None:
        l = l + jnp.exp(sink - m)
    o = jnp.dot(e.astype(BF16), v, preferred_element_type=F32)
    return o * (1.0 / l)


def _head_pair(q, k, v, bias0, bias1, sink0=None, sink1=None):
    lo = lax.broadcasted_iota(jnp.int32, q.shape, 1) < HEAD_DIM
    zero = jnp.zeros_like(q)
    o0 = _attend(jnp.where(lo, q, zero).astype(BF16), k[0], v[0], bias0, sink0)
    o1 = _attend(jnp.where(lo, zero, q).astype(BF16), k[1], v[1], bias1, sink1)
    return jnp.where(lo, o0, o1)


def _dup_halves(x):
    lo = lax.broadcasted_iota(jnp.int32, x.shape, 1) < HEAD_DIM
    sw = pltpu.roll(x, HEAD_DIM, axis=1)
    return jnp.where(lo, x, sw), jnp.where(lo, sw, x)


def _lanes(i):
    return slice(i * LANES, (i + 1) * LANES)


def _prompt_kernel(x_ref, mem_ref, gpre_ref, win_ref, gmem_ref, wmem_ref, wout_ref, gpost_ref,
                   biasa_ref, biasb_ref, sink_ref,
                   y_ref, sak_ref, sav_ref, sbk_ref, sbv_ref, smk_ref, smv_ref,
                   z_s, o_s, ka_s, va_s, kb_s, vb_s, mk_s, mv_s):
    t = pl.program_id(1)
    last = pl.num_programs(1) - 1

    @pl.when(t == 0)
    def _mem_kv():
        mn = _rms(mem_ref[0], gmem_ref[...]).astype(BF16)
        mkv = jnp.dot(mn, wmem_ref[...], preferred_element_type=F32)
        smk_ref[0] = mkv[:, :M_W]
        smv_ref[0] = mkv[:, M_W:]
        mk_s[...] = mkv[:, :M_W].astype(BF16)
        mv_s[...] = mkv[:, M_W:].astype(BF16)

    xn = _rms(x_ref[0], gpre_ref[...]).astype(BF16)
    z_s[...] = jnp.dot(xn, win_ref[...], preferred_element_type=F32)

    cur = slice(HIST, HIST + TILE)
    ka_s[cur, :] = z_s[:, C_KA:C_KA + A_W].astype(BF16)
    va_s[cur, :] = z_s[:, C_VA:C_VA + A_W].astype(BF16)
    k0, k1 = _dup_halves(z_s[:, C_KB:C_KB + B_KVW])
    kb_s[cur, _lanes(0)] = k0.astype(BF16)
    kb_s[cur, _lanes(1)] = k1.astype(BF16)
    v0, v1 = _dup_halves(z_s[:, C_VB:C_VB + B_KVW])
    vb_s[cur, _lanes(0)] = v0.astype(BF16)
    vb_s[cur, _lanes(1)] = v1.astype(BF16)

    @pl.when(t == last)
    def _states():
        sak_ref[0] = z_s[TILE - A_REACH:, C_KA:C_KA + A_W]
        sav_ref[0] = z_s[TILE - A_REACH:, C_VA:C_VA + A_W]
        sbk_ref[0] = z_s[TILE - B_REACH:, C_KB:C_KB + B_KVW]
        sbv_ref[0] = z_s[TILE - B_REACH:, C_VB:C_VB + B_KVW]

    def band_attention(first):
        for g in range(TILE // GA_ROWS):
            r0 = g * GA_ROWS
            rows = slice(r0, r0 + GA_ROWS)
            w0 = HIST if first else r0
            keys = slice(w0, r0 + WIN_A)
            bcols = slice(w0 - r0, WIN_A)
            for p in range(A_HEADS // 2):
                q = z_s[rows, C_QA + p * LANES:C_QA + (p + 1) * LANES] * SCALE
                k = ka_s[keys, _lanes(p)]
                v = va_s[keys, _lanes(p)]
                o = _head_pair(q, (k, k), (v, v),
                               biasa_ref[2 * p, :, bcols], biasa_ref[2 * p + 1, :, bcols])
                gate = _silu(z_s[rows, C_GA + p * LANES:C_GA + (p + 1) * LANES])
                o_s[rows, _lanes(p)] = (o * gate).astype(BF16)

        for g in range(TILE // GB_ROWS):
            r0 = g * GB_ROWS
            rows = slice(r0, r0 + GB_ROWS)
            w0 = HIST if (first and g == 0) else HIST + r0 - B_REACH
            keys = slice(w0, HIST + r0 + GB_ROWS)
            bcols = slice(w0 - (HIST + r0 - B_REACH), WIN_B)
            for p in range(B_HEADS // 2):
                q = z_s[rows, C_QB + p * LANES:C_QB + (p + 1) * LANES] * SCALE
                h0, h1 = 2 * p, 2 * p + 1
                kv0, kv1 = h0 // B_GROUP, h1 // B_GROUP
                o = _head_pair(q,
                               (kb_s[keys, _lanes(kv0)], kb_s[keys, _lanes(kv1)]),
                               (vb_s[keys, _lanes(kv0)], vb_s[keys, _lanes(kv1)]),
                               biasb_ref[h0, :, bcols], biasb_ref[h1, :, bcols],
                               sink_ref[h0], sink_ref[h1])
                gate = _silu(z_s[rows, C_GB + p * LANES:C_GB + (p + 1) * LANES])
                o_s[rows, _lanes(A_W // LANES + p)] = (o * gate).astype(BF16)

    @pl.when(t == 0)
    def _first():
        band_attention(True)

    @pl.when(t > 0)
    def _rest():
        band_attention(False)

    for g in range(TILE // GM_ROWS):
        rows = slice(g * GM_ROWS, (g + 1) * GM_ROWS)
        for p in range(M_HEADS // 2):
            q = z_s[rows, C_QM + p * LANES:C_QM + (p + 1) * LANES] * SCALE
            k = mk_s[:, _lanes(p)]
            v = mv_s[:, _lanes(p)]
            o = _head_pair(q, (k, k), (v, v), None, None)
            gate = _silu(z_s[rows, C_GM + p * LANES:C_GM + (p + 1) * LANES])
            o_s[rows, _lanes((A_W + B_W) // LANES + p)] = (o * gate).astype(BF16)

    om = jnp.dot(o_s[...], wout_ref[...], preferred_element_type=F32)
    y_ref[0] = x_ref[0] + _rms(om, gpost_ref[...])

    ka_s[0:HIST, :] = ka_s[TILE:TILE + HIST, :]
    va_s[0:HIST, :] = va_s[TILE:TILE + HIST, :]
    kb_s[0:HIST, :] = kb_s[TILE:TILE + HIST, :]
    vb_s[0:HIST, :] = vb_s[TILE:TILE + HIST, :]


def _const_spec(shape):
    n = len(shape)
    return pl.BlockSpec(shape, lambda *_: (0,) * n, pipeline_mode=pl.Buffered(1))


def _prompt_call(x, mem, gpre, win, gmem, wmem, wout, gpost, biasa, biasb, sink):
    nb, seq, _ = x.shape
    nt = seq // TILE
    in_specs = [
        pl.BlockSpec((1, TILE, D_MODEL), lambda b, t: (b, t, 0)),
        pl.BlockSpec((1, N_MEM, D_MODEL), lambda b, t: (b, 0, 0)),
        _const_spec((1, D_MODEL)),
        _const_spec((D_MODEL, D_IN)),
        _const_spec((1, D_MODEL)),
        _const_spec((D_MODEL, 2 * M_W)),
        _const_spec((D_MIX, D_MODEL)),
        _const_spec((1, D_MODEL)),
        _const_spec((A_HEADS, GA_ROWS, WIN_A)),
        _const_spec((B_HEADS, GB_ROWS, WIN_B)),
        pl.BlockSpec(memory_space=pltpu.SMEM),
    ]
    per_batch = lambda r, c: pl.BlockSpec((1, r, c), lambda b, t: (b, 0, 0))
    out_specs = [
        pl.BlockSpec((1, TILE, D_MODEL), lambda b, t: (b, t, 0)),
        per_batch(A_REACH, A_W), per_batch(A_REACH, A_W),
        per_batch(B_REACH, B_KVW), per_batch(B_REACH, B_KVW),
        per_batch(N_MEM, M_W), per_batch(N_MEM, M_W),
    ]
    out_shape = [
        jax.ShapeDtypeStruct((nb, seq, D_MODEL), F32),
        jax.ShapeDtypeStruct((nb, A_REACH, A_W), F32),
        jax.ShapeDtypeStruct((nb, A_REACH, A_W), F32),
        jax.ShapeDtypeStruct((nb, B_REACH, B_KVW), F32),
        jax.ShapeDtypeStruct((nb, B_REACH, B_KVW), F32),
        jax.ShapeDtypeStruct((nb, N_MEM, M_W), F32),
        jax.ShapeDtypeStruct((nb, N_MEM, M_W), F32),
    ]
    scratch = [
        pltpu.VMEM((TILE, D_IN), F32),
        pltpu.VMEM((TILE, D_MIX), BF16),
        pltpu.VMEM((HIST + TILE, A_W), BF16),
        pltpu.VMEM((HIST + TILE, A_W), BF16),
        pltpu.VMEM((HIST + TILE, 2 * LANES), BF16),
        pltpu.VMEM((HIST + TILE, 2 * LANES), BF16),
        pltpu.VMEM((N_MEM, M_W), BF16),
        pltpu.VMEM((N_MEM, M_W), BF16),
    ]
    return pl.pallas_call(
        _prompt_kernel,
        grid=(nb, nt),
        in_specs=in_specs,
        out_specs=out_specs,
        out_shape=out_shape,
        scratch_shapes=scratch,
        compiler_params=pltpu.CompilerParams(
            dimension_semantics=("arbitrary", "arbitrary"),
            vmem_limit_bytes=VMEM_LIMIT_BYTES),
    )(x, mem, gpre, win, gmem, wmem, wout, gpost, biasa, biasb, sink)


def _sample_kernel(n_new, xs_ref, cak_ref, cav_ref, cbk_ref, cbv_ref, cmk_ref, cmv_ref,
                   gpre_ref, win_ref, wout_ref, gpost_ref, biasa_ref, biasb_ref, sink_ref,
                   ys_ref, aks_ref, avs_ref, bks_ref, bvs_ref,
                   z_s, o_s, ka_s, va_s, kb_s, vb_s):
    b = pl.program_id(0)
    last = pl.num_programs(0) - 1
    la = cak_ref.shape[1]
    lb = cbk_ref.shape[1]

    @pl.when(b == 0)
    def _project():
        xn = _rms(xs_ref[...], gpre_ref[...]).astype(BF16)
        z_s[...] = jnp.dot(xn, win_ref[...], preferred_element_type=F32)
        aks_ref[...] = z_s[:, C_KA:C_KA + A_W]
        avs_ref[...] = z_s[:, C_VA:C_VA + A_W]
        bks_ref[...] = z_s[:, C_KB:C_KB + B_KVW]
        bvs_ref[...] = z_s[:, C_VB:C_VB + B_KVW]
        ka_s[la:, :] = jnp.zeros((S_KEYS_A - la, A_W), BF16)
        va_s[la:, :] = jnp.zeros((S_KEYS_A - la, A_W), BF16)
        kb_s[lb:, :] = jnp.zeros((S_KEYS_B - lb, 2 * LANES), BF16)
        vb_s[lb:, :] = jnp.zeros((S_KEYS_B - lb, 2 * LANES), BF16)

    rows = pl.ds(pl.multiple_of(b * n_new, n_new), n_new)

    ka_s[0:la, :] = cak_ref[0].astype(BF16)
    va_s[0:la, :] = cav_ref[0].astype(BF16)
    ka_s[la:la + n_new, :] = z_s[rows, C_KA:C_KA + A_W].astype(BF16)
    va_s[la:la + n_new, :] = z_s[rows, C_VA:C_VA + A_W].astype(BF16)
    for src_ref, col, dst in ((cbk_ref, C_KB, kb_s), (cbv_ref, C_VB, vb_s)):
        c0, c1 = _dup_halves(src_ref[0])
        dst[0:lb, _lanes(0)] = c0.astype(BF16)
        dst[0:lb, _lanes(1)] = c1.astype(BF16)
        n0, n1 = _dup_halves(z_s[rows, col:col + B_KVW])
        dst[lb:lb + n_new, _lanes(0)] = n0.astype(BF16)
        dst[lb:lb + n_new, _lanes(1)] = n1.astype(BF16)

    for p in range(A_HEADS // 2):
        q = z_s[rows, C_QA + p * LANES:C_QA + (p + 1) * LANES] * SCALE
        k = ka_s[:, _lanes(p)]
        v = va_s[:, _lanes(p)]
        o = _head_pair(q, (k, k), (v, v), biasa_ref[2 * p], biasa_ref[2 * p + 1])
        gate = _silu(z_s[rows, C_GA + p * LANES:C_GA + (p + 1) * LANES])
        o_s[rows, _lanes(p)] = (o * gate).astype(BF16)

    for p in range(B_HEADS // 2):
        q = z_s[rows, C_QB + p * LANES:C_QB + (p + 1) * LANES] * SCALE
        h0, h1 = 2 * p, 2 * p + 1
        kv0, kv1 = h0 // B_GROUP, h1 // B_GROUP
        o = _head_pair(q,
                       (kb_s[:, _lanes(kv0)], kb_s[:, _lanes(kv1)]),
                       (vb_s[:, _lanes(kv0)], vb_s[:, _lanes(kv1)]),
                       biasb_ref[h0], biasb_ref[h1], sink_ref[h0], sink_ref[h1])
        gate = _silu(z_s[rows, C_GB + p * LANES:C_GB + (p + 1) * LANES])
        o_s[rows, _lanes(A_W // LANES + p)] = (o * gate).astype(BF16)

    for p in range(M_HEADS // 2):
        q = z_s[rows, C_QM + p * LANES:C_QM + (p + 1) * LANES] * SCALE
        k = cmk_ref[0, :, _lanes(p)].astype(BF16)
        v = cmv_ref[0, :, _lanes(p)].astype(BF16)
        o = _head_pair(q, (k, k), (v, v), None, None)
        gate = _silu(z_s[rows, C_GM + p * LANES:C_GM + (p + 1) * LANES])
        o_s[rows, _lanes((A_W + B_W) // LANES + p)] = (o * gate).astype(BF16)

    @pl.when(b == last)
    def _merge():
        om = jnp.dot(o_s[...], wout_ref[...], preferred_element_type=F32)
        ys_ref[...] = xs_ref[...] + _rms(om, gpost_ref[...])


def _sample_call(xs, cak, cav, cbk, cbv, cmk, cmv, gpre, win, wout, gpost, biasa, biasb, sink,
                 n_new):
    nb, la, _ = cak.shape
    lb = cbk.shape[1]
    rows = xs.shape[0]
    full = lambda r, c: pl.BlockSpec((r, c), lambda b: (0, 0))
    per_batch = lambda r, c: pl.BlockSpec((1, r, c), lambda b: (b, 0, 0))
    in_specs = [
        full(rows, D_MODEL),
        per_batch(la, A_W), per_batch(la, A_W),
        per_batch(lb, B_KVW), per_batch(lb, B_KVW),
        per_batch(N_MEM, M_W), per_batch(N_MEM, M_W),
        full(1, D_MODEL), full(D_MODEL, D_IN), full(D_MIX, D_MODEL), full(1, D_MODEL),
        pl.BlockSpec((A_HEADS, n_new, S_KEYS_A), lambda b: (0, 0, 0)),
        pl.BlockSpec((B_HEADS, n_new, S_KEYS_B), lambda b: (0, 0, 0)),
        pl.BlockSpec(memory_space=pltpu.SMEM),
    ]
    out_specs = [full(rows, D_MODEL), full(rows, A_W), full(rows, A_W),
                 full(rows, B_KVW), full(rows, B_KVW)]
    out_shape = [
        jax.ShapeDtypeStruct((rows, D_MODEL), F32),
        jax.ShapeDtypeStruct((rows, A_W), F32),
        jax.ShapeDtypeStruct((rows, A_W), F32),
        jax.ShapeDtypeStruct((rows, B_KVW), F32),
        jax.ShapeDtypeStruct((rows, B_KVW), F32),
    ]
    scratch = [
        pltpu.VMEM((rows, D_IN), F32),
        pltpu.VMEM((rows, D_MIX), BF16),
        pltpu.VMEM((S_KEYS_A, A_W), BF16),
        pltpu.VMEM((S_KEYS_A, A_W), BF16),
        pltpu.VMEM((S_KEYS_B, 2 * LANES), BF16),
        pltpu.VMEM((S_KEYS_B, 2 * LANES), BF16),
    ]
    return pl.pallas_call(
        functools.partial(_sample_kernel, n_new),
        grid=(nb,),
        in_specs=in_specs,
        out_specs=out_specs,
        out_shape=out_shape,
        scratch_shapes=scratch,
        compiler_params=pltpu.CompilerParams(
            dimension_semantics=("arbitrary",),
            vmem_limit_bytes=VMEM_LIMIT_BYTES),
    )(xs, cak, cav, cbk, cbv, cmk, cmv, gpre, win, wout, gpost, biasa, biasb, sink)


def _rel_bias(table, d):
    return table[:, jnp.clip(d, -REL_CLIP, REL_CLIP) + REL_CLIP].astype(F32)


def _alibi_slopes():
    return 2.0 ** (-8.0 * jnp.arange(1, B_HEADS + 1, dtype=F32) / B_HEADS)


def _band_tables(table, rows, reach, left):
    i = jnp.arange(rows)[:, None]
    j = jnp.arange(rows + reach)[None, :]
    ci, cj = i // CHUNK, j // CHUNK
    valid = (cj >= ci) & (cj <= ci + left)
    d = i + reach - j
    return d, valid


def kernel(x_prompt, x_sample, cache_a_k, cache_a_v, cache_b_k, cache_b_v, cache_mem_k, cache_mem_v,
           mem_prompt, g_pre, w_in, rel_bias_a, sink_b, g_mem, w_mem_kv, w_out, g_post):
    depth = g_pre.shape[0]
    assert depth == 1, "single-layer step"
    nb, seq, _ = x_prompt.shape
    nsb, n_new, _ = x_sample.shape
    la_s = cache_a_k.shape[2]
    lb_s = cache_b_k.shape[2]
    assert seq % TILE == 0 and seq >= A_REACH
    assert la_s + n_new <= S_KEYS_A and lb_s + n_new <= S_KEYS_B

    l = 0
    win = w_in[l].astype(BF16)
    wout = w_out[l].astype(BF16)
    wmem = w_mem_kv[l].astype(BF16)
    gpre = g_pre[l][None, :]
    gpost = g_post[l][None, :]
    gmem = g_mem[l][None, :]
    sink = sink_b[l].astype(F32)
    slopes = _alibi_slopes()

    d_a, ok_a = _band_tables(None, GA_ROWS, A_REACH, A_LEFT)
    biasa_p = jnp.where(ok_a[None], _rel_bias(rel_bias_a[l], d_a), NEG)
    d_b, ok_b = _band_tables(None, GB_ROWS, B_REACH, B_LEFT)
    biasb_p = jnp.where(ok_b[None], -slopes[:, None, None] * jnp.abs(d_b).astype(F32)[None], NEG)

    i = jnp.arange(n_new)[:, None]
    ja = jnp.arange(S_KEYS_A)[None, :]
    biasa_s = jnp.where((ja < la_s + n_new)[None], _rel_bias(rel_bias_a[l], la_s + i - ja), NEG)
    jb = jnp.arange(S_KEYS_B)[None, :]
    biasb_s = jnp.where((jb < lb_s + n_new)[None],
                        -slopes[:, None, None] * jnp.abs(lb_s + i - jb).astype(F32)[None], NEG)

    yp, sak, sav, sbk, sbv, smk, smv = _prompt_call(
        x_prompt, mem_prompt, gpre, win, gmem, wmem, wout, gpost, biasa_p, biasb_p, sink)

    ys, aks, avs, bks, bvs = _sample_call(
        x_sample.reshape(nsb * n_new, D_MODEL),
        cache_a_k[l].reshape(nsb, la_s, A_W), cache_a_v[l].reshape(nsb, la_s, A_W),
        cache_b_k[l].reshape(nsb, lb_s, B_KVW), cache_b_v[l].reshape(nsb, lb_s, B_KVW),
        cache_mem_k[l].reshape(nsb, N_MEM, M_W), cache_mem_v[l].reshape(nsb, N_MEM, M_W),
        gpre, win, wout, gpost, biasa_s, biasb_s, sink, n_new)

    return (yp,
            ys.reshape(nsb, n_new, D_MODEL),
            sak.reshape(1, nb, A_REACH, A_HEADS, HEAD_DIM),
            sav.reshape(1, nb, A_REACH, A_HEADS, HEAD_DIM),
            sbk.reshape(1, nb, B_REACH, B_KV_HEADS, HEAD_DIM),
            sbv.reshape(1, nb, B_REACH, B_KV_HEADS, HEAD_DIM),
            smk.reshape(1, nb, N_MEM, M_HEADS, HEAD_DIM),
            smv.reshape(1, nb, N_MEM, M_HEADS, HEAD_DIM),
            aks.reshape(1, nsb, n_new, A_HEADS, HEAD_DIM),
            avs.reshape(1, nsb, n_new, A_HEADS, HEAD_DIM),
            bks.reshape(1, nsb, n_new, B_KV_HEADS, HEAD_DIM),
            bvs.reshape(1, nsb, n_new, B_KV_HEADS, HEAD_DIM))
```
